```python
import math
import jax, jax.numpy as jnp
from jax import lax
import numpy as np

D_MODEL = 1024
BATCH = 8
SEQ = 8192
DEPTH = 4

N_HEADS = 16
N_KV_HEADS = 4
HEAD_DIM = 64
GROUP = N_HEADS // N_KV_HEADS
ATTN_WIDTH = N_HEADS * HEAD_DIM
KV_WIDTH = N_KV_HEADS * HEAD_DIM
WINDOW = 128
BLOCK = 128
ATTN_IN = 2 * ATTN_WIDTH + 2 * KV_WIDTH
POOL_WIDTH = D_MODEL
POOL_WINDOWS = (2, 4, 8, 16)
N_POOL_GROUPS = len(POOL_WINDOWS)
POOL_GC = POOL_WIDTH // N_POOL_GROUPS
POOL_IN = 2 * POOL_WIDTH
N_MIXERS = 2
N_A = (DEPTH + 1) // 2
N_B = DEPTH // 2
EPS = 1e-6

kernel_name = "hybrid_swa_sink_multiscale_pool"


def rmsnorm(x, g):
    xf = x.astype(jnp.float32)
    y = xf * lax.rsqrt(jnp.mean(xf * xf, axis=-1, keepdims=True) + EPS)
    return (y * g.astype(jnp.float32)).astype(x.dtype)


def swa_with_sinks(q, k, v, sinks):
    B, S = q.shape[0], q.shape[1]
    nb = S // BLOCK
    qb = q.reshape(B, nb, BLOCK, N_KV_HEADS, GROUP, HEAD_DIM)
    def band(t):
        tb = t.reshape(B, nb, BLOCK, N_KV_HEADS, HEAD_DIM)
        prev = jnp.pad(tb, ((0, 0), (1, 0), (0, 0), (0, 0), (0, 0)))[:, :-1]
        return jnp.concatenate([prev, tb], axis=2)
    kc, vc = band(k), band(v)
    s = jnp.einsum('bnqhgd,bnkhd->bnhgqk', qb, kc).astype(jnp.float32)
    s = s * (1.0 / math.sqrt(HEAD_DIM))
    qi = jnp.arange(BLOCK)[:, None]
    kj = jnp.arange(2 * BLOCK)[None, :]
    diff = qi + BLOCK - kj
    in_band = (diff >= 0) & (diff < WINDOW)
    not_pad = (jnp.arange(nb)[:, None, None] > 0) | (kj[None] >= BLOCK)
    valid = in_band[None] & not_pad
    s = jnp.where(valid[None, :, None, None], s, -jnp.inf)
    sink = sinks.astype(jnp.float32).reshape(N_KV_HEADS, GROUP)[:, :, None, None]
    m = jnp.maximum(jnp.max(s, axis=-1, keepdims=True), sink)
    p = jnp.exp(s - m)
    denom = jnp.sum(p, axis=-1, keepdims=True) + jnp.exp(sink - m)
    p = (p / denom).astype(v.dtype)
    o = jnp.einsum('bnhgqk,bnkhd->bnqhgd', p, vc)
    return o.reshape(B, S, ATTN_WIDTH)


def attn_layer(h, w_in, sinks, w_out):
    B, S, _ = h.shape
    proj = h @ w_in
    q, k, v, z = jnp.split(proj, [ATTN_WIDTH, ATTN_WIDTH + KV_WIDTH, ATTN_WIDTH + 2 * KV_WIDTH], axis=-1)
    q = q.reshape(B, S, N_KV_HEADS, GROUP, HEAD_DIM)
    k = k.reshape(B, S, N_KV_HEADS, HEAD_DIM)
    v = v.reshape(B, S, N_KV_HEADS, HEAD_DIM)
    o = swa_with_sinks(q, k, v, sinks)
    return (o * jax.nn.silu(z)) @ w_out


def multiscale_pool(u):
    B, S, _ = u.shape
    uf = u.astype(jnp.float32).reshape(B, S, N_POOL_GROUPS, POOL_GC)
    cs = jnp.pad(jnp.cumsum(uf, axis=1), ((0, 0), (1, 0), (0, 0), (0, 0)))
    outs = []
    for gi, w in enumerate(POOL_WINDOWS):
        c = cs[:, :, gi]
        lower = jnp.pad(c[:, :S + 1 - w], ((0, 0), (w - 1, 0), (0, 0)))
        count = jnp.minimum(jnp.arange(1, S + 1), w).astype(jnp.float32)[None, :, None]
        outs.append((c[:, 1:] - lower) / count - uf[:, :, gi])
    return jnp.stack(outs, axis=2).astype(u.dtype)


def pool_layer(h, w_in, w_mix, scale, w_out):
    B, S, _ = h.shape
    u, z = jnp.split(h @ w_in, [POOL_WIDTH], axis=-1)
    p = multiscale_pool(u)
    m = jnp.einsum('bsgc,gcd->bsgd', p, w_mix).reshape(B, S, POOL_WIDTH) * scale
    return (m * jax.nn.silu(z)) @ w_out


def setup_inputs(seed: int = 0) -> dict:
    key = jax.random.key(seed)
    ks = jax.random.split(key, 12)
    out_scale = 1.0 / math.sqrt(2.0 * DEPTH)
    x = jax.random.normal(ks[0], (BATCH, SEQ, D_MODEL), jnp.float32)
    norm_g = 1.0 + 0.05 * jax.random.normal(ks[1], (DEPTH, D_MODEL), jnp.float32)
    attn_w_in = jax.random.normal(ks[2], (N_A, D_MODEL, ATTN_IN), jnp.float32) * D_MODEL ** -0.5
    attn_sinks = 0.5 * jax.random.normal(ks[3], (N_A, N_HEADS), jnp.float32)
    attn_w_out = jax.random.normal(ks[4], (N_A, ATTN_WIDTH, D_MODEL), jnp.float32) * ATTN_WIDTH ** -0.5 * out_scale
    pool_w_in = jax.random.normal(ks[5], (N_B, D_MODEL, POOL_IN), jnp.float32) * D_MODEL ** -0.5
    pool_w_mix = jax.random.normal(ks[6], (N_B, N_POOL_GROUPS, POOL_GC, POOL_GC), jnp.float32) * POOL_GC ** -0.5
    pool_scale = 1.0 + 0.1 * jax.random.normal(ks[7], (N_B, POOL_WIDTH), jnp.float32)
    pool_w_out = jax.random.normal(ks[8], (N_B, POOL_WIDTH, D_MODEL), jnp.float32) * POOL_WIDTH ** -0.5 * out_scale
    final_g = 1.0 + 0.05 * jax.random.normal(ks[9], (D_MODEL,), jnp.float32)
    return {"x": x, "norm_g": norm_g, "attn_w_in": attn_w_in, "attn_sinks": attn_sinks,
            "attn_w_out": attn_w_out, "pool_w_in": pool_w_in, "pool_w_mix": pool_w_mix,
            "pool_scale": pool_scale, "pool_w_out": pool_w_out, "final_g": final_g}


def reference(x, norm_g, attn_w_in, attn_sinks, attn_w_out, pool_w_in, pool_w_mix,
              pool_scale, pool_w_out, final_g):
    for i in range(DEPTH):
        h = rmsnorm(x, norm_g[i])
        j = i // N_MIXERS
        if i % N_MIXERS == 0:
            y = attn_layer(h, attn_w_in[j], attn_sinks[j], attn_w_out[j])
        else:
            y = pool_layer(h, pool_w_in[j], pool_w_mix[j], pool_scale[j], pool_w_out[j])
        x = x + y.astype(x.dtype)
    return rmsnorm(x, final_g)
```

```python
import functools
import math

import jax
import jax.numpy as jnp
from jax import lax
from jax.experimental import pallas as pl
from jax.experimental.pallas import tpu as pltpu

D_MODEL = 1024
N_HEADS = 16
N_KV_HEADS = 4
HEAD_DIM = 64
GROUP = N_HEADS // N_KV_HEADS
ATTN_WIDTH = N_HEADS * HEAD_DIM
KV_WIDTH = N_KV_HEADS * HEAD_DIM
WINDOW = 128
ATTN_IN = 2 * ATTN_WIDTH + 2 * KV_WIDTH
POOL_WIDTH = D_MODEL
POOL_WINDOWS = (2, 4, 8, 16)
POOL_GC = POOL_WIDTH // len(POOL_WINDOWS)
POOL_IN = 2 * POOL_WIDTH
POOL_HALO = 16
EPS = 1e-6

LANES = 128
SEQ_TILE = 512
VMEM_LIMIT_BYTES = 56 * 1024 * 1024

F32 = jnp.float32
BF16 = jnp.bfloat16


def _rmsnorm(x, g):
    ms = jnp.mean(x * x, axis=-1, keepdims=True)
    return (x * lax.rsqrt(ms + EPS)) * g


def _silu(z):
    return z * (1.0 / (1.0 + jnp.exp(-z)))


def _dot(a, b):
    return jnp.dot(a, b, preferred_element_type=F32)


def _dot_nt(a, b):
    return lax.dot_general(a, b, (((1,), (1,)), ((), ())), preferred_element_type=F32)


def _attn_kernel(x_ref, g_ref, win_ref, sink_ref, wout_ref, o_ref,
                 q_s, z_s, klo_s, khi_s, vlo_s, vhi_s, og_s):
    T = SEQ_TILE
    s_idx = pl.program_id(1)

    @pl.when(s_idx == 0)
    def _():
        zeros = jnp.zeros((N_KV_HEADS, WINDOW, LANES), BF16)
        klo_s[:, 0:WINDOW, :] = zeros
        khi_s[:, 0:WINDOW, :] = zeros
        vlo_s[:, 0:WINDOW, :] = zeros
        vhi_s[:, 0:WINDOW, :] = zeros

    x = x_ref[0]
    h = _rmsnorm(x, g_ref[...]).astype(BF16)

    scale = 1.0 / math.sqrt(HEAD_DIM)
    q_s[...] = (_dot(h, win_ref[:, 0:ATTN_WIDTH]) * scale).astype(BF16)
    z_s[...] = _dot(h, win_ref[:, ATTN_WIDTH + 2 * KV_WIDTH:ATTN_IN])

    lane = lax.broadcasted_iota(jnp.int32, (T, LANES), 1)
    low_half = lane < HEAD_DIM
    for (c0, lo_s, hi_s) in ((ATTN_WIDTH, klo_s, khi_s), (ATTN_WIDTH + KV_WIDTH, vlo_s, vhi_s)):
        kv = _dot(h, win_ref[:, c0:c0 + KV_WIDTH])
        for g in range(N_KV_HEADS):
            col = kv[:, (g // 2) * LANES:(g // 2 + 1) * LANES]
            if g % 2 == 0:
                lo = jnp.where(low_half, col, 0.0)
                hi = pltpu.roll(lo, HEAD_DIM, 1)
            else:
                hi = jnp.where(low_half, 0.0, col)
                lo = pltpu.roll(hi, HEAD_DIM, 1)
            lo_s[g, WINDOW:WINDOW + T, :] = lo.astype(BF16)
            hi_s[g, WINDOW:WINDOW + T, :] = hi.astype(BF16)

    qi = lax.broadcasted_iota(jnp.int32, (WINDOW, 2 * WINDOW), 0)
    kj = lax.broadcasted_iota(jnp.int32, (WINDOW, 2 * WINDOW), 1)
    band = (kj > qi) & (kj <= qi + WINDOW)
    band_first = band & ((kj >= WINDOW) | (s_idx > 0))
    neg_inf = -jnp.inf

    for i in range(T // WINDOW):
        valid = band_first if i == 0 else band
        r0 = i * WINDOW
        for g in range(N_KV_HEADS):
            qs = jnp.concatenate(
                [q_s[r0:r0 + WINDOW, (2 * g) * LANES:(2 * g + 1) * LANES],
                 q_s[r0:r0 + WINDOW, (2 * g + 1) * LANES:(2 * g + 2) * LANES]], axis=0)
            p_halves = []
            for b, k_s in enumerate((klo_s, khi_s)):
                s_all = _dot_nt(qs, k_s[g, r0:r0 + 2 * WINDOW, :])
                p_rows = []
                for a in range(2):
                    head = GROUP * g + 2 * a + b
                    sink = sink_ref[head]
                    s = jnp.where(valid, s_all[a * WINDOW:(a + 1) * WINDOW], neg_inf)
                    m = jnp.maximum(jnp.max(s, axis=-1, keepdims=True), sink)
                    p = jnp.exp(s - m)
                    denom = jnp.sum(p, axis=-1, keepdims=True) + jnp.exp(sink - m)
                    p_rows.append((p * (1.0 / denom)).astype(BF16))
                p_halves.append(jnp.concatenate(p_rows, axis=0))
            o = (_dot(p_halves[0], vlo_s[g, r0:r0 + 2 * WINDOW, :]) +
                 _dot(p_halves[1], vhi_s[g, r0:r0 + 2 * WINDOW, :]))
            for a in range(2):
                c = (2 * g + a) * LANES
                zc = z_s[r0:r0 + WINDOW, c:c + LANES]
                og_s[r0:r0 + WINDOW, c:c + LANES] = (
                    o[a * WINDOW:(a + 1) * WINDOW] * _silu(zc)).astype(BF16)

    for buf in (klo_s, khi_s, vlo_s, vhi_s):
        buf[:, 0:WINDOW, :] = buf[:, T:T + WINDOW, :]

    y = _dot(og_s[...], wout_ref[...])
    o_ref[0] = x_ref[0] + y


def _attn_layer(x, g, w_in, sinks, w_out):
    B, S, D = x.shape
    T = SEQ_TILE
    const = lambda b, s: (0, 0)
    return pl.pallas_call(
        _attn_kernel,
        out_shape=jax.ShapeDtypeStruct((B, S, D), F32),
        grid=(B, S // T),
        in_specs=[
            pl.BlockSpec((1, T, D), lambda b, s: (b, s, 0)),
            pl.BlockSpec((1, D), const),
            pl.BlockSpec((D, ATTN_IN), const),
            pl.BlockSpec(memory_space=pltpu.SMEM),
            pl.BlockSpec((ATTN_WIDTH, D), const),
        ],
        out_specs=pl.BlockSpec((1, T, D), lambda b, s: (b, s, 0)),
        scratch_shapes=[
            pltpu.VMEM((T, ATTN_WIDTH), BF16),
            pltpu.VMEM((T, ATTN_WIDTH), F32),
            pltpu.VMEM((N_KV_HEADS, WINDOW + T, LANES), BF16),
            pltpu.VMEM((N_KV_HEADS, WINDOW + T, LANES), BF16),
            pltpu.VMEM((N_KV_HEADS, WINDOW + T, LANES), BF16),
            pltpu.VMEM((N_KV_HEADS, WINDOW + T, LANES), BF16),
            pltpu.VMEM((T, ATTN_WIDTH), BF16),
        ],
        compiler_params=pltpu.CompilerParams(
            dimension_semantics=("arbitrary", "arbitrary"),
            vmem_limit_bytes=VMEM_LIMIT_BYTES),
        name="attn_layer",
    )(x, g, w_in, sinks, w_out)


def _pool_kernel(x_ref, g_ref, win_ref, wmix_ref, scale_ref, wout_ref, fg_ref, o_ref,
                 u_s, *, final_norm):
    T = SEQ_TILE
    s_idx = pl.program_id(1)

    @pl.when(s_idx == 0)
    def _():
        u_s[0:POOL_HALO, :] = jnp.zeros((POOL_HALO, POOL_WIDTH), F32)

    x = x_ref[0]
    h = _rmsnorm(x, g_ref[...]).astype(BF16)
    u = _dot(h, win_ref[:, 0:POOL_WIDTH])
    z = _dot(h, win_ref[:, POOL_WIDTH:POOL_IN])
    u_s[POOL_HALO:POOL_HALO + T, :] = u

    pos = s_idx * T + lax.broadcasted_iota(jnp.int32, (T, 1), 0)
    gated = []
    for gi, w in enumerate(POOL_WINDOWS):
        c0 = gi * POOL_GC
        ue = u_s[:, c0:c0 + POOL_GC]
        acc = ue
        step = 1
        while step < w:
            acc = acc + pltpu.roll(acc, step, 0)
            step *= 2
        inv_count = 1.0 / jnp.minimum(pos + 1, w).astype(F32)
        pooled = acc[POOL_HALO:] * inv_count - ue[POOL_HALO:]
        mixed = _dot(pooled.astype(BF16), wmix_ref[gi]) * scale_ref[:, c0:c0 + POOL_GC]
        gated.append((mixed * _silu(z[:, c0:c0 + POOL_GC])).astype(BF16))

    u_s[0:POOL_HALO, :] = u_s[T:T + POOL_HALO, :]

    y = _dot(jnp.concatenate(gated, axis=1), wout_ref[...])
    out = x_ref[0] + y
    if final_norm:
        out = _rmsnorm(out, fg_ref[...])
    o_ref[0] = out


def _pool_layer(x, g, w_in, w_mix, scale, w_out, final_g, final_norm):
    B, S, D = x.shape
    T = SEQ_TILE
    const = lambda b, s: (0, 0)
    return pl.pallas_call(
        functools.partial(_pool_kernel, final_norm=final_norm),
        out_shape=jax.ShapeDtypeStruct((B, S, D), F32),
        grid=(B, S // T),
        in_specs=[
            pl.BlockSpec((1, T, D), lambda b, s: (b, s, 0)),
            pl.BlockSpec((1, D), const),
            pl.BlockSpec((D, POOL_IN), const),
            pl.BlockSpec((len(POOL_WINDOWS), POOL_GC, POOL_GC), lambda b, s: (0, 0, 0)),
            pl.BlockSpec((1, POOL_WIDTH), const),
            pl.BlockSpec((POOL_WIDTH, D), const),
            pl.BlockSpec((1, D), const),
        ],
        out_specs=pl.BlockSpec((1, T, D), lambda b, s: (b, s, 0)),
        scratch_shapes=[pltpu.VMEM((POOL_HALO + T, POOL_WIDTH), F32)],
        compiler_params=pltpu.CompilerParams(
            dimension_semantics=("arbitrary", "arbitrary"),
            vmem_limit_bytes=VMEM_LIMIT_BYTES),
        name="pool_final" if final_norm else "pool_layer",
    )(x, g, w_in, w_mix, scale, w_out, final_g)


def kernel(x, norm_g, attn_w_in, attn_sinks, attn_w_out, pool_w_in, pool_w_mix,
           pool_scale, pool_w_out, final_g):
    depth = norm_g.shape[0]
    attn_w_in = attn_w_in.astype(BF16)
    attn_w_out = attn_w_out.astype(BF16)
    pool_w_in = pool_w_in.astype(BF16)
    pool_w_mix = pool_w_mix.astype(BF16)
    pool_w_out = pool_w_out.astype(BF16)
    fg = final_g.reshape(1, D_MODEL)
    for i in range(depth):
        g = norm_g[i].reshape(1, D_MODEL)
        j = i // 2
        if i % 2 == 0:
            x = _attn_layer(x, g, attn_w_in[j], attn_sinks[j], attn_w_out[j])
        else:
            x = _pool_layer(x, g, pool_w_in[j], pool_w_mix[j],
                            pool_scale[j].reshape(1, POOL_WIDTH), pool_w_out[j], fg,
                            final_norm=(i == depth - 1))
    return x
```

```python
import functools
import math

import jax
import jax.numpy as jnp
from jax import lax
from jax.experimental import pallas as pl
from jax.experimental.pallas import tpu as pltpu

D_MODEL = 1024
N_HEADS = 16
N_KV_HEADS = 4
HEAD_DIM = 64
GROUP = N_HEADS // N_KV_HEADS
ATTN_WIDTH = N_HEADS * HEAD_DIM
KV_WIDTH = N_KV_HEADS * HEAD_DIM
WINDOW = 128
ATTN_IN = 2 * ATTN_WIDTH + 2 * KV_WIDTH
POOL_WIDTH = D_MODEL
POOL_WINDOWS = (2, 4, 8, 16)
POOL_GC = POOL_WIDTH // len(POOL_WINDOWS)
POOL_IN = 2 * POOL_WIDTH
POOL_HALO = 16
EPS = 1e-6

LANES = 128
BF16_SUBLANES = 16
SEQ_TILE = 512
VMEM_LIMIT_BYTES = 56 * 1024 * 1024

F32 = jnp.float32
BF16 = jnp.bfloat16


def _rmsnorm(x, g):
    ms = jnp.mean(x * x, axis=-1, keepdims=True)
    return (x * lax.rsqrt(ms + EPS)) * g


def _silu(z):
    return z * (1.0 / (1.0 + jnp.exp(-z)))


def _dot(a, b):
    return jnp.dot(a, b, preferred_element_type=F32)


def _dot_nt(a, b):
    return lax.dot_general(a, b, (((1,), (1,)), ((), ())), preferred_element_type=F32)


def _attn_kernel(x_ref, g_ref, win_ref, sink_ref, wout_ref, o_ref,
                 q_s, z_s, klo_s, khi_s, vlo_s, vhi_s, og_s, s_buf, p_buf):
    T = SEQ_TILE
    n_blocks = T // WINDOW
    s_idx = pl.program_id(1)

    @pl.when(s_idx == 0)
    def _():
        zeros = jnp.zeros((N_KV_HEADS, WINDOW, LANES), BF16)
        klo_s[:, 0:WINDOW, :] = zeros
        khi_s[:, 0:WINDOW, :] = zeros
        vlo_s[:, 0:WINDOW, :] = zeros
        vhi_s[:, 0:WINDOW, :] = zeros

    x = x_ref[0]
    h = _rmsnorm(x, g_ref[...]).astype(BF16)

    scale = 1.0 / math.sqrt(HEAD_DIM)
    n_cols = ATTN_WIDTH // LANES
    q = _dot(h, win_ref[:, 0:ATTN_WIDTH])
    for c in range(n_cols):
        q_s[c] = (q[:, c * LANES:(c + 1) * LANES] * scale).astype(BF16)
    z = _dot(h, win_ref[:, ATTN_WIDTH + 2 * KV_WIDTH:ATTN_IN])
    for c in range(n_cols):
        z_s[c] = z[:, c * LANES:(c + 1) * LANES]

    low_half_t = lax.broadcasted_iota(jnp.int32, (T, LANES), 1) < HEAD_DIM
    for (c0, lo_s, hi_s) in ((ATTN_WIDTH, klo_s, khi_s), (ATTN_WIDTH + KV_WIDTH, vlo_s, vhi_s)):
        kv = _dot(h, win_ref[:, c0:c0 + KV_WIDTH])
        for g in range(N_KV_HEADS):
            col = kv[:, (g // 2) * LANES:(g // 2 + 1) * LANES]
            if g % 2 == 0:
                lo = jnp.where(low_half_t, col, 0.0)
                hi = pltpu.roll(lo, HEAD_DIM, 1)
            else:
                hi = jnp.where(low_half_t, 0.0, col)
                lo = pltpu.roll(hi, HEAD_DIM, 1)
            lo_s[g, WINDOW:WINDOW + T, :] = lo.astype(BF16)
            hi_s[g, WINDOW:WINDOW + T, :] = hi.astype(BF16)

    neg_inf = -jnp.inf

    def stage_a(r0, g, slot):
        qs = jnp.concatenate([q_s[2 * g, pl.ds(r0, WINDOW), :],
                              q_s[2 * g + 1, pl.ds(r0, WINDOW), :]], axis=0)
        for b, k_s in enumerate((klo_s, khi_s)):
            s_buf[slot, b] = _dot_nt(qs, k_s[g, pl.ds(r0, 2 * WINDOW), :])

    def stage_b(g, slot, seq_start):
        row = lax.broadcasted_iota(jnp.int32, (WINDOW, WINDOW), 0)
        col = lax.broadcasted_iota(jnp.int32, (WINDOW, WINDOW), 1)
        prev_valid = col > row
        if seq_start is not None:
            prev_valid = prev_valid & jnp.logical_not(seq_start)
        cur_valid = col <= row
        col0 = lax.broadcasted_iota(jnp.int32, (1, WINDOW), 1) == 0
        for b in range(2):
            for a in range(2):
                sink = sink_ref[GROUP * g + 2 * a + b]
                s = s_buf[slot, b, a * WINDOW:(a + 1) * WINDOW, :]
                s0 = jnp.where(prev_valid, s[:, :WINDOW], jnp.where(col0, sink, neg_inf))
                s1 = jnp.where(cur_valid, s[:, WINDOW:], neg_inf)
                m = jnp.max(jnp.maximum(s0, s1), axis=-1, keepdims=True)
                p = jnp.concatenate([jnp.exp(s0 - m), jnp.exp(s1 - m)], axis=1)
                p_buf[slot, b, a * WINDOW:(a + 1) * WINDOW, :] = p.astype(BF16)

    def stage_c(r0, g, slot):
        sub = BF16_SUBLANES
        lane = lax.broadcasted_iota(jnp.int32, (2 * WINDOW, LANES), 1)
        first_row = lax.broadcasted_iota(jnp.int32, (sub, LANES), 0) == 0
        r = None
        for b, v_s in enumerate((vlo_s, vhi_s)):
            top = v_s[g, pl.ds(r0, sub), :]
            top = jnp.where(first_row, jnp.zeros_like(top), top)
            rest = v_s[g, pl.ds(r0 + sub, 2 * WINDOW - sub), :]
            ones = jnp.where((lane < HEAD_DIM) == (b == 0), 1.0, 0.0).astype(BF16)
            rhs = jnp.concatenate([jnp.concatenate([top, rest], axis=0), ones], axis=1)
            part = _dot(p_buf[slot, b], rhs)
            r = part if r is None else r + part
        o = r[:, :LANES] * (1.0 / r[:, LANES:])
        for a in range(2):
            c = 2 * g + a
            zc = z_s[c, pl.ds(r0, WINDOW), :]
            og_s[c, pl.ds(r0, WINDOW), :] = (
                o[a * WINDOW:(a + 1) * WINDOW] * _silu(zc)).astype(BF16)

    def block(i, first, last):
        if isinstance(i, int):
            r0 = i * WINDOW
        else:
            r0 = pl.multiple_of(i * WINDOW, WINDOW)
        seq_start = (s_idx == 0) if first else None
        for g in range(N_KV_HEADS):
            if g + 1 < N_KV_HEADS:
                stage_a(r0, g + 1, (g + 1) % 2)
            elif not last:
                stage_a(r0 + WINDOW, 0, 0)
            stage_b(g, g % 2, seq_start)
            if g > 0:
                stage_c(r0, g - 1, (g - 1) % 2)
            elif not first:
                stage_c(r0 - WINDOW, N_KV_HEADS - 1, (N_KV_HEADS - 1) % 2)

    stage_a(0, 0, 0)
    block(0, True, n_blocks == 1)
    if n_blocks > 2:
        def body(i, carry):
            block(i, False, False)
            return carry
        lax.fori_loop(1, n_blocks - 1, body, 0)
    if n_blocks > 1:
        block(n_blocks - 1, False, True)
    stage_c((n_blocks - 1) * WINDOW, N_KV_HEADS - 1, (N_KV_HEADS - 1) % 2)

    for buf in (klo_s, khi_s, vlo_s, vhi_s):
        buf[:, 0:WINDOW, :] = buf[:, T:T + WINDOW, :]

    og = jnp.concatenate([og_s[c] for c in range(n_cols)], axis=1)
    o_ref[0] = x_ref[0] + _dot(og, wout_ref[...])


def _attn_layer(x, g, w_in, sinks, w_out):
    B, S, D = x.shape
    T = SEQ_TILE
    const = lambda b, s: (0, 0)
    return pl.pallas_call(
        _attn_kernel,
        out_shape=jax.ShapeDtypeStruct((B, S, D), F32),
        grid=(B, S // T),
        in_specs=[
            pl.BlockSpec((1, T, D), lambda b, s: (b, s, 0)),
            pl.BlockSpec((1, D), const),
            pl.BlockSpec((D, ATTN_IN), const),
            pl.BlockSpec(memory_space=pltpu.SMEM),
            pl.BlockSpec((ATTN_WIDTH, D), const),
        ],
        out_specs=pl.BlockSpec((1, T, D), lambda b, s: (b, s, 0)),
        scratch_shapes=[
            pltpu.VMEM((ATTN_WIDTH // LANES, T, LANES), BF16),
            pltpu.VMEM((ATTN_WIDTH // LANES, T, LANES), F32),
            pltpu.VMEM((N_KV_HEADS, WINDOW + T, LANES), BF16),
            pltpu.VMEM((N_KV_HEADS, WINDOW + T, LANES), BF16),
            pltpu.VMEM((N_KV_HEADS, WINDOW + T, LANES), BF16),
            pltpu.VMEM((N_KV_HEADS, WINDOW + T, LANES), BF16),
            pltpu.VMEM((ATTN_WIDTH // LANES, T, LANES), BF16),
            pltpu.VMEM((2, 2, 2 * WINDOW, 2 * WINDOW), F32),
            pltpu.VMEM((2, 2, 2 * WINDOW, 2 * WINDOW), BF16),
        ],
        compiler_params=pltpu.CompilerParams(
            dimension_semantics=("arbitrary", "arbitrary"),
            vmem_limit_bytes=VMEM_LIMIT_BYTES),
        name="attn_layer",
    )(x, g, w_in, sinks, w_out)


def _pool_kernel(x_ref, g_ref, win_ref, wmix_ref, scale_ref, wout_ref, fg_ref, o_ref,
                 u_s, *, final_norm):
    T = SEQ_TILE
    s_idx = pl.program_id(1)

    @pl.when(s_idx == 0)
    def _():
        u_s[0:POOL_HALO, :] = jnp.zeros((POOL_HALO, POOL_WIDTH), F32)

    x = x_ref[0]
    h = _rmsnorm(x, g_ref[...]).astype(BF16)
    u = _dot(h, win_ref[:, 0:POOL_WIDTH])
    z = _dot(h, win_ref[:, POOL_WIDTH:POOL_IN])
    u_s[POOL_HALO:POOL_HALO + T, :] = u

    pos = s_idx * T + lax.broadcasted_iota(jnp.int32, (T, 1), 0)
    gated = []
    for gi, w in enumerate(POOL_WINDOWS):
        c0 = gi * POOL_GC
        ue = u_s[:, c0:c0 + POOL_GC]
        acc = ue
        step = 1
        while step < w:
            acc = acc + pltpu.roll(acc, step, 0)
            step *= 2
        inv_count = 1.0 / jnp.minimum(pos + 1, w).astype(F32)
        pooled = acc[POOL_HALO:] * inv_count - ue[POOL_HALO:]
        mixed = _dot(pooled.astype(BF16), wmix_ref[gi]) * scale_ref[:, c0:c0 + POOL_GC]
        gated.append((mixed * _silu(z[:, c0:c0 + POOL_GC])).astype(BF16))

    u_s[0:POOL_HALO, :] = u_s[T:T + POOL_HALO, :]

    y = _dot(jnp.concatenate(gated, axis=1), wout_ref[...])
    out = x_ref[0] + y
    if final_norm:
        out = _rmsnorm(out, fg_ref[...])
    o_ref[0] = out


def _pool_layer(x, g, w_in, w_mix, scale, w_out, final_g, final_norm):
    B, S, D = x.shape
    T = SEQ_TILE
    const = lambda b, s: (0, 0)
    return pl.pallas_call(
        functools.partial(_pool_kernel, final_norm=final_norm),
        out_shape=jax.ShapeDtypeStruct((B, S, D), F32),
        grid=(B, S // T),
        in_specs=[
            pl.BlockSpec((1, T, D), lambda b, s: (b, s, 0)),
            pl.BlockSpec((1, D), const),
            pl.BlockSpec((D, POOL_IN), const),
            pl.BlockSpec((len(POOL_WINDOWS), POOL_GC, POOL_GC), lambda b, s: (0, 0, 0)),
            pl.BlockSpec((1, POOL_WIDTH), const),
            pl.BlockSpec((POOL_WIDTH, D), const),
            pl.BlockSpec((1, D), const),
        ],
        out_specs=pl.BlockSpec((1, T, D), lambda b, s: (b, s, 0)),
        scratch_shapes=[pltpu.VMEM((POOL_HALO + T, POOL_WIDTH), F32)],
        compiler_params=pltpu.CompilerParams(
            dimension_semantics=("arbitrary", "arbitrary"),
            vmem_limit_bytes=VMEM_LIMIT_BYTES),
        name="pool_final" if final_norm else "pool_layer",
    )(x, g, w_in, w_mix, scale, w_out, final_g)


def kernel(x, norm_g, attn_w_in, attn_sinks, attn_w_out, pool_w_in, pool_w_mix,
           pool_scale, pool_w_out, final_g):
    depth = norm_g.shape[0]
    assert depth % 2 == 0, "the final rmsnorm is fused into the last (pooling) layer"
    attn_w_in = attn_w_in.astype(BF16)
    attn_w_out = attn_w_out.astype(BF16)
    pool_w_in = pool_w_in.astype(BF16)
    pool_w_mix = pool_w_mix.astype(BF16)
    pool_w_out = pool_w_out.astype(BF16)
    fg = final_g.reshape(1, D_MODEL)
    for i in range(depth):
        g = norm_g[i].reshape(1, D_MODEL)
        j = i // 2
        if i % 2 == 0:
            x = _attn_layer(x, g, attn_w_in[j], attn_sinks[j], attn_w_out[j])
        else:
            x = _pool_layer(x, g, pool_w_in[j], pool_w_mix[j],
                            pool_scale[j].reshape(1, POOL_WIDTH), pool_w_out[j], fg,
                            final_norm=(i == depth - 1))
    return x
```

```python
import functools
import math

import jax
import jax.numpy as jnp
from jax import lax
from jax.experimental import pallas as pl
from jax.experimental.pallas import tpu as pltpu

D_MODEL = 1024
N_HEADS = 16
N_KV_HEADS = 4
HEAD_DIM = 64
GROUP = N_HEADS // N_KV_HEADS
ATTN_WIDTH = N_HEADS * HEAD_DIM
KV_WIDTH = N_KV_HEADS * HEAD_DIM
WINDOW = 128
ATTN_IN = 2 * ATTN_WIDTH + 2 * KV_WIDTH
POOL_WIDTH = D_MODEL
POOL_WINDOWS = (2, 4, 8, 16)
POOL_GC = POOL_WIDTH // len(POOL_WINDOWS)
POOL_IN = 2 * POOL_WIDTH
POOL_HALO = 16
EPS = 1e-6
LOG2_E = math.log2(math.e)

LANES = 128
BF16_SUBLANES = 16
STAGE_DIST = 2
SEQ_TILE = 512
VMEM_LIMIT_BYTES = 56 * 1024 * 1024

F32 = jnp.float32
BF16 = jnp.bfloat16


def _rmsnorm(x, g):
    ms = jnp.mean(x * x, axis=-1, keepdims=True)
    return (x * lax.rsqrt(ms + EPS)) * g


def _silu(z):
    return z * (1.0 / (1.0 + jnp.exp(-z)))


def _dot(a, b):
    return jnp.dot(a, b, preferred_element_type=F32)


def _dot_nt(a, b):
    return lax.dot_general(a, b, (((1,), (1,)), ((), ())), preferred_element_type=F32)


def _attn_kernel(x_ref, g_ref, win_ref, sink_ref, wout_ref, o_ref,
                 q_s, z_s, klo_s, khi_s, vlo_s, vhi_s, og_s, s_buf, p_buf):
    T = SEQ_TILE
    n_blocks = T // WINDOW
    s_idx = pl.program_id(1)

    @pl.when(s_idx == 0)
    def _():
        zeros = jnp.zeros((N_KV_HEADS, WINDOW, LANES), BF16)
        klo_s[:, 0:WINDOW, :] = zeros
        khi_s[:, 0:WINDOW, :] = zeros
        vlo_s[:, 0:WINDOW, :] = zeros
        vhi_s[:, 0:WINDOW, :] = zeros

    x = x_ref[0]
    h = _rmsnorm(x, g_ref[...]).astype(BF16)

    scale = LOG2_E / math.sqrt(HEAD_DIM)
    n_cols = ATTN_WIDTH // LANES
    q = _dot(h, win_ref[:, 0:ATTN_WIDTH])
    for c in range(n_cols):
        q_s[c] = (q[:, c * LANES:(c + 1) * LANES] * scale).astype(BF16)
    z = _dot(h, win_ref[:, ATTN_WIDTH + 2 * KV_WIDTH:ATTN_IN])
    for c in range(n_cols):
        z_s[c] = _silu(z[:, c * LANES:(c + 1) * LANES])

    low_half_t = lax.broadcasted_iota(jnp.int32, (T, LANES), 1) < HEAD_DIM
    for (c0, lo_s, hi_s) in ((ATTN_WIDTH, klo_s, khi_s), (ATTN_WIDTH + KV_WIDTH, vlo_s, vhi_s)):
        kv = _dot(h, win_ref[:, c0:c0 + KV_WIDTH])
        for g in range(N_KV_HEADS):
            col = kv[:, (g // 2) * LANES:(g // 2 + 1) * LANES]
            if g % 2 == 0:
                lo = jnp.where(low_half_t, col, 0.0)
                hi = pltpu.roll(lo, HEAD_DIM, 1)
            else:
                hi = jnp.where(low_half_t, 0.0, col)
                lo = pltpu.roll(hi, HEAD_DIM, 1)
            lo_s[g, WINDOW:WINDOW + T, :] = lo.astype(BF16)
            hi_s[g, WINDOW:WINDOW + T, :] = hi.astype(BF16)

    neg_inf = -jnp.inf

    def stage_a(r0, g, slot):
        qs = jnp.concatenate([q_s[2 * g, pl.ds(r0, WINDOW), :],
                              q_s[2 * g + 1, pl.ds(r0, WINDOW), :]], axis=0)
        for b, k_s in enumerate((klo_s, khi_s)):
            s_buf[slot, b] = _dot_nt(qs, k_s[g, pl.ds(r0, 2 * WINDOW), :])

    def stage_b(g, slot, seq_start):
        row = lax.broadcasted_iota(jnp.int32, (WINDOW, WINDOW), 0)
        col = lax.broadcasted_iota(jnp.int32, (WINDOW, WINDOW), 1)
        prev_valid = col > row
        if seq_start is not None:
            prev_valid = prev_valid & jnp.logical_not(seq_start)
        cur_valid = col <= row
        col0 = lax.broadcasted_iota(jnp.int32, (1, WINDOW), 1) == 0
        for b in range(2):
            for a in range(2):
                sink = sink_ref[GROUP * g + 2 * a + b] * LOG2_E
                s = s_buf[slot, b, a * WINDOW:(a + 1) * WINDOW, :]
                s0 = jnp.where(prev_valid, s[:, :WINDOW], jnp.where(col0, sink, neg_inf))
                s1 = jnp.where(cur_valid, s[:, WINDOW:], neg_inf)
                m = jnp.max(jnp.maximum(s0, s1), axis=-1, keepdims=True)
                p = jnp.concatenate([jnp.exp2(s0 - m), jnp.exp2(s1 - m)], axis=1)
                p_buf[slot, b, a * WINDOW:(a + 1) * WINDOW, :] = p.astype(BF16)

    def stage_c(r0, g, slot):
        sub = BF16_SUBLANES
        lane = lax.broadcasted_iota(jnp.int32, (2 * WINDOW, LANES), 1)
        first_row = lax.broadcasted_iota(jnp.int32, (sub, LANES), 0) == 0
        r = None
        for b, v_s in enumerate((vlo_s, vhi_s)):
            top = v_s[g, pl.ds(r0, sub), :]
            top = jnp.where(first_row, jnp.zeros_like(top), top)
            rest = v_s[g, pl.ds(r0 + sub, 2 * WINDOW - sub), :]
            ones = jnp.where((lane < HEAD_DIM) == (b == 0), 1.0, 0.0).astype(BF16)
            rhs = jnp.concatenate([jnp.concatenate([top, rest], axis=0), ones], axis=1)
            part = _dot(p_buf[slot, b], rhs)
            r = part if r is None else r + part
        o = r[:, :LANES] * (1.0 / r[:, LANES:])
        for a in range(2):
            c = 2 * g + a
            gate = z_s[c, pl.ds(r0, WINDOW), :]
            og_s[c, pl.ds(r0, WINDOW), :] = (o[a * WINDOW:(a + 1) * WINDOW] * gate).astype(BF16)

    def block(i, first, last):
        if isinstance(i, int):
            r0 = i * WINDOW
        else:
            r0 = pl.multiple_of(i * WINDOW, WINDOW)
        seq_start = (s_idx == 0) if first else None
        for g in range(N_KV_HEADS):
            ga = g + STAGE_DIST
            if ga < N_KV_HEADS:
                stage_a(r0, ga, ga)
            elif not last:
                stage_a(r0 + WINDOW, ga - N_KV_HEADS, ga - N_KV_HEADS)
            stage_b(g, g, seq_start)
            gc = g - STAGE_DIST
            if gc >= 0:
                stage_c(r0, gc, gc)
            elif not first:
                stage_c(r0 - WINDOW, gc + N_KV_HEADS, gc + N_KV_HEADS)

    for g in range(STAGE_DIST):
        stage_a(0, g, g)
    block(0, True, n_blocks == 1)
    for i in range(1, n_blocks - 1):
        block(i, False, False)
    if n_blocks > 1:
        block(n_blocks - 1, False, True)
    for g in range(N_KV_HEADS - STAGE_DIST, N_KV_HEADS):
        stage_c((n_blocks - 1) * WINDOW, g, g)

    for buf in (klo_s, khi_s, vlo_s, vhi_s):
        buf[:, 0:WINDOW, :] = buf[:, T:T + WINDOW, :]

    og = jnp.concatenate([og_s[c] for c in range(n_cols)], axis=1)
    o_ref[0] = x_ref[0] + _dot(og, wout_ref[...])


def _attn_layer(x, g, w_in, sinks, w_out):
    B, S, D = x.shape
    T = SEQ_TILE
    const = lambda b, s: (0, 0)
    return pl.pallas_call(
        _attn_kernel,
        out_shape=jax.ShapeDtypeStruct((B, S, D), F32),
        grid=(B, S // T),
        in_specs=[
            pl.BlockSpec((1, T, D), lambda b, s: (b, s, 0)),
            pl.BlockSpec((1, D), const),
            pl.BlockSpec((D, ATTN_IN), const),
            pl.BlockSpec(memory_space=pltpu.SMEM),
            pl.BlockSpec((ATTN_WIDTH, D), const),
        ],
        out_specs=pl.BlockSpec((1, T, D), lambda b, s: (b, s, 0)),
        scratch_shapes=[
            pltpu.VMEM((ATTN_WIDTH // LANES, T, LANES), BF16),
            pltpu.VMEM((ATTN_WIDTH // LANES, T, LANES), F32),
            pltpu.VMEM((N_KV_HEADS, WINDOW + T, LANES), BF16),
            pltpu.VMEM((N_KV_HEADS, WINDOW + T, LANES), BF16),
            pltpu.VMEM((N_KV_HEADS, WINDOW + T, LANES), BF16),
            pltpu.VMEM((N_KV_HEADS, WINDOW + T, LANES), BF16),
            pltpu.VMEM((ATTN_WIDTH // LANES, T, LANES), BF16),
            pltpu.VMEM((N_KV_HEADS, 2, 2 * WINDOW, 2 * WINDOW), F32),
            pltpu.VMEM((N_KV_HEADS, 2, 2 * WINDOW, 2 * WINDOW), BF16),
        ],
        compiler_params=pltpu.CompilerParams(
            dimension_semantics=("arbitrary", "arbitrary"),
            vmem_limit_bytes=VMEM_LIMIT_BYTES),
        name="attn_layer",
    )(x, g, w_in, sinks, w_out)


def _pool_kernel(x_ref, g_ref, win_ref, wmix_ref, scale_ref, wout_ref, fg_ref, o_ref,
                 u_s, *, final_norm):
    T = SEQ_TILE
    s_idx = pl.program_id(1)

    @pl.when(s_idx == 0)
    def _():
        u_s[0:POOL_HALO, :] = jnp.zeros((POOL_HALO, POOL_WIDTH), F32)

    x = x_ref[0]
    h = _rmsnorm(x, g_ref[...]).astype(BF16)
    u = _dot(h, win_ref[:, 0:POOL_WIDTH])
    z = _dot(h, win_ref[:, POOL_WIDTH:POOL_IN])
    u_s[POOL_HALO:POOL_HALO + T, :] = u

    pos = s_idx * T + lax.broadcasted_iota(jnp.int32, (T, 1), 0)
    gated = []
    for gi, w in enumerate(POOL_WINDOWS):
        c0 = gi * POOL_GC
        ue = u_s[:, c0:c0 + POOL_GC]
        acc = ue
        step = 1
        while step < w:
            acc = acc + pltpu.roll(acc, step, 0)
            step *= 2
        inv_count = 1.0 / jnp.minimum(pos + 1, w).astype(F32)
        pooled = acc[POOL_HALO:] * inv_count - ue[POOL_HALO:]
        mixed = _dot(pooled.astype(BF16), wmix_ref[gi]) * scale_ref[:, c0:c0 + POOL_GC]
        gated.append((mixed * _silu(z[:, c0:c0 + POOL_GC])).astype(BF16))

    u_s[0:POOL_HALO, :] = u_s[T:T + POOL_HALO, :]

    y = _dot(jnp.concatenate(gated, axis=1), wout_ref[...])
    out = x_ref[0] + y
    if final_norm:
        out = _rmsnorm(out, fg_ref[...])
    o_ref[0] = out


def _pool_layer(x, g, w_in, w_mix, scale, w_out, final_g, final_norm):
    B, S, D = x.shape
    T = SEQ_TILE
    const = lambda b, s: (0, 0)
    return pl.pallas_call(
        functools.partial(_pool_kernel, final_norm=final_norm),
        out_shape=jax.ShapeDtypeStruct((B, S, D), F32),
        grid=(B, S // T),
        in_specs=[
            pl.BlockSpec((1, T, D), lambda b, s: (b, s, 0)),
            pl.BlockSpec((1, D), const),
            pl.BlockSpec((D, POOL_IN), const),
            pl.BlockSpec((len(POOL_WINDOWS), POOL_GC, POOL_GC), lambda b, s: (0, 0, 0)),
            pl.BlockSpec((1, POOL_WIDTH), const),
            pl.BlockSpec((POOL_WIDTH, D), const),
            pl.BlockSpec((1, D), const),
        ],
        out_specs=pl.BlockSpec((1, T, D), lambda b, s: (b, s, 0)),
        scratch_shapes=[pltpu.VMEM((POOL_HALO + T, POOL_WIDTH), F32)],
        compiler_params=pltpu.CompilerParams(
            dimension_semantics=("arbitrary", "arbitrary"),
            vmem_limit_bytes=VMEM_LIMIT_BYTES),
        name="pool_final" if final_norm else "pool_layer",
    )(x, g, w_in, w_mix, scale, w_out, final_g)


def kernel(x, norm_g, attn_w_in, attn_sinks, attn_w_out, pool_w_in, pool_w_mix,
           pool_scale, pool_w_out, final_g):
    depth = norm_g.shape[0]
    assert depth % 2 == 0, "the final rmsnorm is fused into the last (pooling) layer"
    attn_w_in = attn_w_in.astype(BF16)
    attn_w_out = attn_w_out.astype(BF16)
    pool_w_in = pool_w_in.astype(BF16)
    pool_w_mix = pool_w_mix.astype(BF16)
    pool_w_out = pool_w_out.astype(BF16)
    fg = final_g.reshape(1, D_MODEL)
    for i in range(depth):
        g = norm_g[i].reshape(1, D_MODEL)
        j = i // 2
        if i % 2 == 0:
            x = _attn_layer(x, g, attn_w_in[j], attn_sinks[j], attn_w_out[j])
        else:
            x = _pool_layer(x, g, pool_w_in[j], pool_w_mix[j],
                            pool_scale[j].reshape(1, POOL_WIDTH), pool_w_out[j], fg,
                            final_norm=(i == depth - 1))
    return x
```

```python
import functools
import math

import jax
import jax.numpy as jnp
from jax import lax
from jax.experimental import pallas as pl
from jax.experimental.pallas import tpu as pltpu

D_MODEL = 1024
N_HEADS = 16
N_KV_HEADS = 4
HEAD_DIM = 64
GROUP = N_HEADS // N_KV_HEADS
ATTN_WIDTH = N_HEADS * HEAD_DIM
KV_WIDTH = N_KV_HEADS * HEAD_DIM
WINDOW = 128
ATTN_IN = 2 * ATTN_WIDTH + 2 * KV_WIDTH
POOL_WIDTH = D_MODEL
POOL_WINDOWS = (2, 4, 8, 16)
POOL_GC = POOL_WIDTH // len(POOL_WINDOWS)
POOL_IN = 2 * POOL_WIDTH
POOL_HALO = 16
EPS = 1e-6
LOG2_E = math.log2(math.e)

LANES = 128
BF16_SUBLANES = 16
STAGE_DIST = 2
SEQ_TILE = 1024
VMEM_LIMIT_BYTES = 56 * 1024 * 1024

F32 = jnp.float32
BF16 = jnp.bfloat16


def _rmsnorm(x, g):
    ms = jnp.mean(x * x, axis=-1, keepdims=True)
    return (x * lax.rsqrt(ms + EPS)) * g


def _silu(z):
    return z * (1.0 / (1.0 + jnp.exp(-z)))


def _dot(a, b):
    return jnp.dot(a, b, preferred_element_type=F32)


def _dot_nt(a, b):
    return lax.dot_general(a, b, (((1,), (1,)), ((), ())), preferred_element_type=F32)


def _attn_kernel(x_ref, g_ref, win_ref, sink_ref, wout_ref, o_ref,
                 q_s, z_s, klo_s, khi_s, vlo_s, vhi_s, og_s, s_buf, p_buf):
    T = SEQ_TILE
    n_blocks = T // WINDOW
    s_idx = pl.program_id(1)

    @pl.when(s_idx == 0)
    def _():
        zeros = jnp.zeros((N_KV_HEADS, WINDOW, LANES), BF16)
        klo_s[:, 0:WINDOW, :] = zeros
        khi_s[:, 0:WINDOW, :] = zeros
        vlo_s[:, 0:WINDOW, :] = zeros
        vhi_s[:, 0:WINDOW, :] = zeros

    x = x_ref[0]
    h = _rmsnorm(x, g_ref[...]).astype(BF16)

    scale = LOG2_E / math.sqrt(HEAD_DIM)
    n_cols = ATTN_WIDTH // LANES
    q = _dot(h, win_ref[:, 0:ATTN_WIDTH])
    for c in range(n_cols):
        q_s[c] = (q[:, c * LANES:(c + 1) * LANES] * scale).astype(BF16)
    z = _dot(h, win_ref[:, ATTN_WIDTH + 2 * KV_WIDTH:ATTN_IN])
    for c in range(n_cols):
        z_s[c] = _silu(z[:, c * LANES:(c + 1) * LANES])

    low_half_t = lax.broadcasted_iota(jnp.int32, (T, LANES), 1) < HEAD_DIM
    for (c0, lo_s, hi_s) in ((ATTN_WIDTH, klo_s, khi_s), (ATTN_WIDTH + KV_WIDTH, vlo_s, vhi_s)):
        kv = _dot(h, win_ref[:, c0:c0 + KV_WIDTH])
        for g in range(N_KV_HEADS):
            col = kv[:, (g // 2) * LANES:(g // 2 + 1) * LANES]
            if g % 2 == 0:
                lo = jnp.where(low_half_t, col, 0.0)
                hi = pltpu.roll(lo, HEAD_DIM, 1)
            else:
                hi = jnp.where(low_half_t, 0.0, col)
                lo = pltpu.roll(hi, HEAD_DIM, 1)
            lo_s[g, WINDOW:WINDOW + T, :] = lo.astype(BF16)
            hi_s[g, WINDOW:WINDOW + T, :] = hi.astype(BF16)

    neg_inf = -jnp.inf

    def stage_a(r0, g, slot):
        qs = jnp.concatenate([q_s[2 * g, pl.ds(r0, WINDOW), :],
                              q_s[2 * g + 1, pl.ds(r0, WINDOW), :]], axis=0)
        for b, k_s in enumerate((klo_s, khi_s)):
            s_buf[slot, b] = _dot_nt(qs, k_s[g, pl.ds(r0, 2 * WINDOW), :])

    def stage_b(g, slot, seq_start):
        row = lax.broadcasted_iota(jnp.int32, (WINDOW, WINDOW), 0)
        col = lax.broadcasted_iota(jnp.int32, (WINDOW, WINDOW), 1)
        prev_valid = col > row
        if seq_start is not None:
            prev_valid = prev_valid & jnp.logical_not(seq_start)
        cur_valid = col <= row
        col0 = lax.broadcasted_iota(jnp.int32, (1, WINDOW), 1) == 0
        for b in range(2):
            for a in range(2):
                sink = sink_ref[GROUP * g + 2 * a + b] * LOG2_E
                s = s_buf[slot, b, a * WINDOW:(a + 1) * WINDOW, :]
                s0 = jnp.where(prev_valid, s[:, :WINDOW], jnp.where(col0, sink, neg_inf))
                s1 = jnp.where(cur_valid, s[:, WINDOW:], neg_inf)
                m = jnp.max(jnp.maximum(s0, s1), axis=-1, keepdims=True)
                p = jnp.concatenate([jnp.exp2(s0 - m), jnp.exp2(s1 - m)], axis=1)
                p_buf[slot, b, a * WINDOW:(a + 1) * WINDOW, :] = p.astype(BF16)

    def stage_c(r0, g, slot):
        sub = BF16_SUBLANES
        lane = lax.broadcasted_iota(jnp.int32, (2 * WINDOW, LANES), 1)
        first_row = lax.broadcasted_iota(jnp.int32, (sub, LANES), 0) == 0
        r = None
        for b, v_s in enumerate((vlo_s, vhi_s)):
            top = v_s[g, pl.ds(r0, sub), :]
            top = jnp.where(first_row, jnp.zeros_like(top), top)
            rest = v_s[g, pl.ds(r0 + sub, 2 * WINDOW - sub), :]
            ones = jnp.where((lane < HEAD_DIM) == (b == 0), 1.0, 0.0).astype(BF16)
            rhs = jnp.concatenate([jnp.concatenate([top, rest], axis=0), ones], axis=1)
            part = _dot(p_buf[slot, b], rhs)
            r = part if r is None else r + part
        o = r[:, :LANES] * (1.0 / r[:, LANES:])
        for a in range(2):
            c = 2 * g + a
            gate = z_s[c, pl.ds(r0, WINDOW), :]
            og_s[c, pl.ds(r0, WINDOW), :] = (o[a * WINDOW:(a + 1) * WINDOW] * gate).astype(BF16)

    def block(i, first, last):
        if isinstance(i, int):
            r0 = i * WINDOW
        else:
            r0 = pl.multiple_of(i * WINDOW, WINDOW)
        seq_start = (s_idx == 0) if first else None
        for g in range(N_KV_HEADS):
            ga = g + STAGE_DIST
            if ga < N_KV_HEADS:
                stage_a(r0, ga, ga)
            elif not last:
                stage_a(r0 + WINDOW, ga - N_KV_HEADS, ga - N_KV_HEADS)
            stage_b(g, g, seq_start)
            gc = g - STAGE_DIST
            if gc >= 0:
                stage_c(r0, gc, gc)
            elif not first:
                stage_c(r0 - WINDOW, gc + N_KV_HEADS, gc + N_KV_HEADS)

    for g in range(STAGE_DIST):
        stage_a(0, g, g)
    block(0, True, n_blocks == 1)
    for i in range(1, n_blocks - 1):
        block(i, False, False)
    if n_blocks > 1:
        block(n_blocks - 1, False, True)
    for g in range(N_KV_HEADS - STAGE_DIST, N_KV_HEADS):
        stage_c((n_blocks - 1) * WINDOW, g, g)

    for buf in (klo_s, khi_s, vlo_s, vhi_s):
        buf[:, 0:WINDOW, :] = buf[:, T:T + WINDOW, :]

    og = jnp.concatenate([og_s[c] for c in range(n_cols)], axis=1)
    o_ref[0] = x_ref[0] + _dot(og, wout_ref[...])


def _attn_layer(x, g, w_in, sinks, w_out):
    B, S, D = x.shape
    T = SEQ_TILE
    const = lambda b, s: (0, 0)
    return pl.pallas_call(
        _attn_kernel,
        out_shape=jax.ShapeDtypeStruct((B, S, D), F32),
        grid=(B, S // T),
        in_specs=[
            pl.BlockSpec((1, T, D), lambda b, s: (b, s, 0)),
            pl.BlockSpec((1, D), const),
            pl.BlockSpec((D, ATTN_IN), const, pipeline_mode=pl.Buffered(1)),
            pl.BlockSpec(memory_space=pltpu.SMEM),
            pl.BlockSpec((ATTN_WIDTH, D), const, pipeline_mode=pl.Buffered(1)),
        ],
        out_specs=pl.BlockSpec((1, T, D), lambda b, s: (b, s, 0)),
        scratch_shapes=[
            pltpu.VMEM((ATTN_WIDTH // LANES, T, LANES), BF16),
            pltpu.VMEM((ATTN_WIDTH // LANES, T, LANES), F32),
            pltpu.VMEM((N_KV_HEADS, WINDOW + T, LANES), BF16),
            pltpu.VMEM((N_KV_HEADS, WINDOW + T, LANES), BF16),
            pltpu.VMEM((N_KV_HEADS, WINDOW + T, LANES), BF16),
            pltpu.VMEM((N_KV_HEADS, WINDOW + T, LANES), BF16),
            pltpu.VMEM((ATTN_WIDTH // LANES, T, LANES), BF16),
            pltpu.VMEM((N_KV_HEADS, 2, 2 * WINDOW, 2 * WINDOW), F32),
            pltpu.VMEM((N_KV_HEADS, 2, 2 * WINDOW, 2 * WINDOW), BF16),
        ],
        compiler_params=pltpu.CompilerParams(
            dimension_semantics=("arbitrary", "arbitrary"),
            vmem_limit_bytes=VMEM_LIMIT_BYTES),
        name="attn_layer",
    )(x, g, w_in, sinks, w_out)


def _pool_kernel(x_ref, g_ref, win_ref, wmix_ref, scale_ref, wout_ref, fg_ref, o_ref,
                 u_s, *, final_norm):
    T = SEQ_TILE
    s_idx = pl.program_id(1)

    @pl.when(s_idx == 0)
    def _():
        u_s[0:POOL_HALO, :] = jnp.zeros((POOL_HALO, POOL_WIDTH), F32)

    x = x_ref[0]
    h = _rmsnorm(x, g_ref[...]).astype(BF16)
    u = _dot(h, win_ref[:, 0:POOL_WIDTH])
    z = _dot(h, win_ref[:, POOL_WIDTH:POOL_IN])
    u_s[POOL_HALO:POOL_HALO + T, :] = u

    pos = s_idx * T + lax.broadcasted_iota(jnp.int32, (T, 1), 0)
    gated = []
    for gi, w in enumerate(POOL_WINDOWS):
        c0 = gi * POOL_GC
        ue = u_s[:, c0:c0 + POOL_GC]
        acc = ue
        step = 1
        while step < w:
            acc = acc + pltpu.roll(acc, step, 0)
            step *= 2
        inv_count = 1.0 / jnp.minimum(pos + 1, w).astype(F32)
        pooled = acc[POOL_HALO:] * inv_count - ue[POOL_HALO:]
        mixed = _dot(pooled.astype(BF16), wmix_ref[gi]) * scale_ref[:, c0:c0 + POOL_GC]
        gated.append((mixed * _silu(z[:, c0:c0 + POOL_GC])).astype(BF16))

    u_s[0:POOL_HALO, :] = u_s[T:T + POOL_HALO, :]

    y = _dot(jnp.concatenate(gated, axis=1), wout_ref[...])
    out = x_ref[0] + y
    if final_norm:
        out = _rmsnorm(out, fg_ref[...])
    o_ref[0] = out


def _pool_layer(x, g, w_in, w_mix, scale, w_out, final_g, final_norm):
    B, S, D = x.shape
    T = SEQ_TILE
    const = lambda b, s: (0, 0)
    return pl.pallas_call(
        functools.partial(_pool_kernel, final_norm=final_norm),
        out_shape=jax.ShapeDtypeStruct((B, S, D), F32),
        grid=(B, S // T),
        in_specs=[
            pl.BlockSpec((1, T, D), lambda b, s: (b, s, 0)),
            pl.BlockSpec((1, D), const),
            pl.BlockSpec((D, POOL_IN), const, pipeline_mode=pl.Buffered(1)),
            pl.BlockSpec((len(POOL_WINDOWS), POOL_GC, POOL_GC), lambda b, s: (0, 0, 0)),
            pl.BlockSpec((1, POOL_WIDTH), const),
            pl.BlockSpec((POOL_WIDTH, D), const, pipeline_mode=pl.Buffered(1)),
            pl.BlockSpec((1, D), const),
        ],
        out_specs=pl.BlockSpec((1, T, D), lambda b, s: (b, s, 0)),
        scratch_shapes=[pltpu.VMEM((POOL_HALO + T, POOL_WIDTH), F32)],
        compiler_params=pltpu.CompilerParams(
            dimension_semantics=("arbitrary", "arbitrary"),
            vmem_limit_bytes=VMEM_LIMIT_BYTES),
        name="pool_final" if final_norm else "pool_layer",
    )(x, g, w_in, w_mix, scale, w_out, final_g)


def kernel(x, norm_g, attn_w_in, attn_sinks, attn_w_out, pool_w_in, pool_w_mix,
           pool_scale, pool_w_out, final_g):
    depth = norm_g.shape[0]
    assert depth % 2 == 0, "the final rmsnorm is fused into the last (pooling) layer"
    attn_w_in = attn_w_in.astype(BF16)
    attn_w_out = attn_w_out.astype(BF16)
    pool_w_in = pool_w_in.astype(BF16)
    pool_w_mix = pool_w_mix.astype(BF16)
    pool_w_out = pool_w_out.astype(BF16)
    fg = final_g.reshape(1, D_MODEL)
    for i in range(depth):
        g = norm_g[i].reshape(1, D_MODEL)
        j = i // 2
        if i % 2 == 0:
            x = _attn_layer(x, g, attn_w_in[j], attn_sinks[j], attn_w_out[j])
        else:
            x = _pool_layer(x, g, pool_w_in[j], pool_w_mix[j],
                            pool_scale[j].reshape(1, POOL_WIDTH), pool_w_out[j], fg,
                            final_norm=(i == depth - 1))
    return x
```

```python
import functools
import math

import jax
import jax.numpy as jnp
from jax import lax
from jax.experimental import pallas as pl
from jax.experimental.pallas import tpu as pltpu

D_MODEL = 1024
N_HEADS = 16
N_KV_HEADS = 4
HEAD_DIM = 64
GROUP = N_HEADS // N_KV_HEADS
ATTN_WIDTH = N_HEADS * HEAD_DIM
KV_WIDTH = N_KV_HEADS * HEAD_DIM
WINDOW = 128
ATTN_IN = 2 * ATTN_WIDTH + 2 * KV_WIDTH
POOL_WIDTH = D_MODEL
POOL_WINDOWS = (2, 4, 8, 16)
POOL_GC = POOL_WIDTH // len(POOL_WINDOWS)
POOL_IN = 2 * POOL_WIDTH
POOL_HALO = 16
EPS = 1e-6
LOG2_E = math.log2(math.e)

LANES = 128
F32_SUBLANES = 8
BF16_SUBLANES = 16
V_ROWS = 2 * HEAD_DIM
STAGE_DIST = 2
POOL_SUB = 512
OUT_ROWS = 256
SEQ_TILE = 1024
VMEM_LIMIT_BYTES = 56 * 1024 * 1024

F32 = jnp.float32
BF16 = jnp.bfloat16


def _rmsnorm(x, g):
    ms = jnp.mean(x * x, axis=-1, keepdims=True)
    return (x * lax.rsqrt(ms + EPS)) * g


def _silu(z):
    return z * (1.0 / (1.0 + jnp.exp2(z * (-LOG2_E))))


def _dot(a, b):
    return jnp.dot(a, b, preferred_element_type=F32)


def _dot_nt(a, b):
    return lax.dot_general(a, b, (((1,), (1,)), ((), ())), preferred_element_type=F32)


def _attn_kernel(x_ref, g_ref, win_ref, sink_ref, wout_ref, o_ref,
                 q_s, z_s, klo_s, khi_s, vtlo_s, vthi_s, og_s, s_buf, p_buf):
    T = SEQ_TILE
    n_blocks = T // WINDOW
    s_idx = pl.program_id(1)

    @pl.when(s_idx == 0)
    def _():
        zeros = jnp.zeros((N_KV_HEADS, WINDOW, LANES), BF16)
        klo_s[:, 0:WINDOW, :] = zeros
        khi_s[:, 0:WINDOW, :] = zeros
        vtlo_s[:, 0:V_ROWS, 0:WINDOW] = zeros
        vthi_s[:, 0:V_ROWS, 0:WINDOW] = zeros

    aug_row = lax.broadcasted_iota(jnp.int32, (N_KV_HEADS, BF16_SUBLANES, WINDOW + T), 1)
    vtlo_s[:, V_ROWS:V_ROWS + BF16_SUBLANES, :] = jnp.where(aug_row == 0, 1.0, 0.0).astype(BF16)
    vthi_s[:, V_ROWS:V_ROWS + BF16_SUBLANES, :] = jnp.where(aug_row == 1, 1.0, 0.0).astype(BF16)

    x = x_ref[0]
    h = _rmsnorm(x, g_ref[...]).astype(BF16)

    scale = LOG2_E / math.sqrt(HEAD_DIM)
    n_cols = ATTN_WIDTH // LANES
    q = _dot(h, win_ref[:, 0:ATTN_WIDTH])
    for c in range(n_cols):
        q_s[c] = (q[:, c * LANES:(c + 1) * LANES] * scale).astype(BF16)
    z = _dot(h, win_ref[:, ATTN_WIDTH + 2 * KV_WIDTH:ATTN_IN])
    for c in range(n_cols):
        z_s[c] = _silu(z[:, c * LANES:(c + 1) * LANES])

    low_half_t = lax.broadcasted_iota(jnp.int32, (T, LANES), 1) < HEAD_DIM
    k = _dot(h, win_ref[:, ATTN_WIDTH:ATTN_WIDTH + KV_WIDTH])
    for g in range(N_KV_HEADS):
        col = k[:, (g // 2) * LANES:(g // 2 + 1) * LANES]
        if g % 2 == 0:
            lo = jnp.where(low_half_t, col, 0.0)
            hi = pltpu.roll(lo, HEAD_DIM, 1)
        else:
            hi = jnp.where(low_half_t, 0.0, col)
            lo = pltpu.roll(hi, HEAD_DIM, 1)
        klo_s[g, WINDOW:WINDOW + T, :] = lo.astype(BF16)
        khi_s[g, WINDOW:WINDOW + T, :] = hi.astype(BF16)

    v_t = _dot(h, win_ref[:, ATTN_WIDTH + KV_WIDTH:ATTN_WIDTH + 2 * KV_WIDTH]).T
    zeros_half = jnp.zeros((HEAD_DIM, T), BF16)
    for g in range(N_KV_HEADS):
        vg = v_t[g * HEAD_DIM:(g + 1) * HEAD_DIM, :].astype(BF16)
        vtlo_s[g, 0:V_ROWS, WINDOW:WINDOW + T] = jnp.concatenate([vg, zeros_half], axis=0)
        vthi_s[g, 0:V_ROWS, WINDOW:WINDOW + T] = jnp.concatenate([zeros_half, vg], axis=0)

    neg_inf = -jnp.inf

    def stage_a(r0, g, slot):
        qs = jnp.concatenate([q_s[2 * g, pl.ds(r0, WINDOW), :],
                              q_s[2 * g + 1, pl.ds(r0, WINDOW), :]], axis=0)
        for b, k_s in enumerate((klo_s, khi_s)):
            s_buf[slot, b] = _dot_nt(qs, k_s[g, pl.ds(r0, 2 * WINDOW), :])

    def stage_b(g, slot, seq_start):
        row = lax.broadcasted_iota(jnp.int32, (WINDOW, WINDOW), 0)
        col = lax.broadcasted_iota(jnp.int32, (WINDOW, WINDOW), 1)
        prev_valid = col > row
        if seq_start is not None:
            prev_valid = prev_valid & jnp.logical_not(seq_start)
        cur_valid = col <= row
        col0 = lax.broadcasted_iota(jnp.int32, (1, WINDOW), 1) == 0
        for b in range(2):
            for a in range(2):
                sink = sink_ref[GROUP * g + 2 * a + b] * LOG2_E
                s = s_buf[slot, b, a * WINDOW:(a + 1) * WINDOW, :]
                s0 = jnp.where(prev_valid, s[:, :WINDOW], jnp.where(col0, sink, neg_inf))
                s1 = jnp.where(cur_valid, s[:, WINDOW:], neg_inf)
                m = jnp.max(jnp.maximum(s0, s1), axis=-1, keepdims=True)
                p = jnp.concatenate([jnp.exp2(s0 - m), jnp.exp2(s1 - m)], axis=1)
                p_buf[slot, b, a * WINDOW:(a + 1) * WINDOW, :] = p.astype(BF16)

    def stage_c(r0, g, slot):
        m_rows = V_ROWS + BF16_SUBLANES
        sink_key = ((lax.broadcasted_iota(jnp.int32, (m_rows, WINDOW), 1) == 0) &
                    (lax.broadcasted_iota(jnp.int32, (m_rows, WINDOW), 0) < V_ROWS))
        r = None
        for b, vt_s in enumerate((vtlo_s, vthi_s)):
            first = vt_s[g, :, pl.ds(r0, WINDOW)]
            first = jnp.where(sink_key, jnp.zeros_like(first), first)
            lhs = jnp.concatenate([first, vt_s[g, :, pl.ds(r0 + WINDOW, WINDOW)]], axis=1)
            part = _dot_nt(lhs, p_buf[slot, b])
            r = part if r is None else r + part
        inv = 1.0 / r[V_ROWS:V_ROWS + F32_SUBLANES, :]
        o_t = jnp.concatenate([r[0:HEAD_DIM] * inv[0:1], r[HEAD_DIM:V_ROWS] * inv[1:2]], axis=0)
        for a in range(2):
            c = 2 * g + a
            o = o_t[:, a * WINDOW:(a + 1) * WINDOW].T
            gate = z_s[c, pl.ds(r0, WINDOW), :]
            og_s[c, pl.ds(r0, WINDOW), :] = (o * gate).astype(BF16)

    def block(i, first, last):
        if isinstance(i, int):
            r0 = i * WINDOW
        else:
            r0 = pl.multiple_of(i * WINDOW, WINDOW)
        seq_start = (s_idx == 0) if first else None
        for g in range(N_KV_HEADS):
            ga = g + STAGE_DIST
            if ga < N_KV_HEADS:
                stage_a(r0, ga, ga)
            elif not last:
                stage_a(r0 + WINDOW, ga - N_KV_HEADS, ga - N_KV_HEADS)
            stage_b(g, g, seq_start)
            gc = g - STAGE_DIST
            if gc >= 0:
                stage_c(r0, gc, gc)
            elif not first:
                stage_c(r0 - WINDOW, gc + N_KV_HEADS, gc + N_KV_HEADS)

    for g in range(STAGE_DIST):
        stage_a(0, g, g)
    block(0, True, n_blocks == 1)
    for i in range(1, n_blocks - 1):
        block(i, False, False)
    if n_blocks > 1:
        block(n_blocks - 1, False, True)
    for g in range(N_KV_HEADS - STAGE_DIST, N_KV_HEADS):
        stage_c((n_blocks - 1) * WINDOW, g, g)

    for buf in (klo_s, khi_s):
        buf[:, 0:WINDOW, :] = buf[:, T:T + WINDOW, :]
    for buf in (vtlo_s, vthi_s):
        buf[:, 0:V_ROWS, 0:WINDOW] = buf[:, 0:V_ROWS, T:T + WINDOW]

    og = jnp.concatenate([og_s[c] for c in range(n_cols)], axis=1)
    o_ref[0] = x_ref[0] + _dot(og, wout_ref[...])


def _attn_layer(x, g, w_in, sinks, w_out):
    B, S, D = x.shape
    T = SEQ_TILE
    const = lambda b, s: (0, 0)
    return pl.pallas_call(
        _attn_kernel,
        out_shape=jax.ShapeDtypeStruct((B, S, D), F32),
        grid=(B, S // T),
        in_specs=[
            pl.BlockSpec((1, T, D), lambda b, s: (b, s, 0)),
            pl.BlockSpec((1, D), const),
            pl.BlockSpec((D, ATTN_IN), const, pipeline_mode=pl.Buffered(1)),
            pl.BlockSpec(memory_space=pltpu.SMEM),
            pl.BlockSpec((ATTN_WIDTH, D), const, pipeline_mode=pl.Buffered(1)),
        ],
        out_specs=pl.BlockSpec((1, T, D), lambda b, s: (b, s, 0)),
        scratch_shapes=[
            pltpu.VMEM((ATTN_WIDTH // LANES, T, LANES), BF16),
            pltpu.VMEM((ATTN_WIDTH // LANES, T, LANES), F32),
            pltpu.VMEM((N_KV_HEADS, WINDOW + T, LANES), BF16),
            pltpu.VMEM((N_KV_HEADS, WINDOW + T, LANES), BF16),
            pltpu.VMEM((N_KV_HEADS, V_ROWS + BF16_SUBLANES, WINDOW + T), BF16),
            pltpu.VMEM((N_KV_HEADS, V_ROWS + BF16_SUBLANES, WINDOW + T), BF16),
            pltpu.VMEM((ATTN_WIDTH // LANES, T, LANES), BF16),
            pltpu.VMEM((N_KV_HEADS, 2, 2 * WINDOW, 2 * WINDOW), F32),
            pltpu.VMEM((N_KV_HEADS, 2, 2 * WINDOW, 2 * WINDOW), BF16),
        ],
        compiler_params=pltpu.CompilerParams(
            dimension_semantics=("arbitrary", "arbitrary"),
            vmem_limit_bytes=VMEM_LIMIT_BYTES),
        name="attn_layer",
    )(x, g, w_in, sinks, w_out)


def _pool_kernel(x_ref, g_ref, win_ref, wmix_ref, scale_ref, wout_ref, fg_ref, o_ref,
                 u_s, *, final_norm):
    T = SEQ_TILE
    s_idx = pl.program_id(1)

    @pl.when(s_idx == 0)
    def _():
        u_s[0:POOL_HALO, :] = jnp.zeros((POOL_HALO, POOL_WIDTH), F32)

    x = x_ref[0]
    h = _rmsnorm(x, g_ref[...]).astype(BF16)

    assert T == 2 * POOL_SUB
    n_groups = len(POOL_WINDOWS)
    head_pos = lax.broadcasted_iota(jnp.int32, (POOL_HALO, 1), 0)

    def cols(gi):
        return slice(gi * POOL_GC, (gi + 1) * POOL_GC)

    def u_proj(k, gi):
        rows = slice(POOL_HALO + k * POOL_SUB, POOL_HALO + (k + 1) * POOL_SUB)
        u_s[rows, cols(gi)] = _dot(h[k * POOL_SUB:(k + 1) * POOL_SUB], win_ref[:, cols(gi)])

    def z_proj(k, gi):
        c0 = POOL_WIDTH + gi * POOL_GC
        return _dot(h[k * POOL_SUB:(k + 1) * POOL_SUB], win_ref[:, c0:c0 + POOL_GC])

    def pool(k, gi):
        w = POOL_WINDOWS[gi]
        r0 = k * POOL_SUB
        ue = u_s[r0:r0 + POOL_HALO + POOL_SUB, cols(gi)]
        acc = ue
        step = 1
        while step < w:
            acc = acc + pltpu.roll(acc, step, 0)
            step *= 2
        sums = acc[POOL_HALO:]
        if k == 0:
            count = jnp.minimum(head_pos + 1, w).astype(F32)
            head_factor = jnp.where(s_idx == 0, float(w) / count, 1.0)
            sums = jnp.concatenate([sums[0:POOL_HALO] * head_factor, sums[POOL_HALO:]], axis=0)
        pooled = sums * (1.0 / w) - ue[POOL_HALO:]
        return pooled.astype(BF16)

    def mix(gi, pooled):
        return _dot(pooled, wmix_ref[gi]) * scale_ref[:, cols(gi)]

    def gate(mixed, z):
        return (mixed * _silu(z)).astype(BF16)

    def out_proj(k, gated, r):
        rows = slice(k * POOL_SUB + r, k * POOL_SUB + r + OUT_ROWS)
        out = x_ref[0, rows, :] + _dot(gated[r:r + OUT_ROWS], wout_ref[...])
        if final_norm:
            out = _rmsnorm(out, fg_ref[...])
        o_ref[0, rows, :] = out

    for gi in range(n_groups):
        u_proj(0, gi)
    z0, pooled0, mixed0 = [], [], []
    for gi in range(n_groups):
        z0.append(z_proj(0, gi))
        pooled0.append(pool(0, gi))
        if gi > 0:
            mixed0.append(mix(gi - 1, pooled0[gi - 1]))
    gated0 = []
    for gi in range(n_groups):
        u_proj(1, gi)
        if gi == 0:
            mixed0.append(mix(n_groups - 1, pooled0[n_groups - 1]))
        gated0.append(gate(mixed0[gi], z0[gi]))
    gated0 = jnp.concatenate(gated0, axis=1)
    z1, pooled1, mixed1 = [], [], []
    for gi in range(n_groups):
        z1.append(z_proj(1, gi))
        pooled1.append(pool(1, gi))
        if gi > 0:
            mixed1.append(mix(gi - 1, pooled1[gi - 1]))
    gated1 = []
    chunks = list(range(0, POOL_SUB, OUT_ROWS))
    per_chunk = n_groups // len(chunks)
    for ci, r in enumerate(chunks):
        out_proj(0, gated0, r)
        if ci == 0:
            mixed1.append(mix(n_groups - 1, pooled1[n_groups - 1]))
        for gi in range(ci * per_chunk, (ci + 1) * per_chunk):
            gated1.append(gate(mixed1[gi], z1[gi]))
    gated1 = jnp.concatenate(gated1, axis=1)
    for r in chunks:
        out_proj(1, gated1, r)

    u_s[0:POOL_HALO, :] = u_s[T:T + POOL_HALO, :]


def _pool_layer(x, g, w_in, w_mix, scale, w_out, final_g, final_norm):
    B, S, D = x.shape
    T = SEQ_TILE
    const = lambda b, s: (0, 0)
    return pl.pallas_call(
        functools.partial(_pool_kernel, final_norm=final_norm),
        out_shape=jax.ShapeDtypeStruct((B, S, D), F32),
        grid=(B, S // T),
        in_specs=[
            pl.BlockSpec((1, T, D), lambda b, s: (b, s, 0)),
            pl.BlockSpec((1, D), const),
            pl.BlockSpec((D, POOL_IN), const, pipeline_mode=pl.Buffered(1)),
            pl.BlockSpec((len(POOL_WINDOWS), POOL_GC, POOL_GC), lambda b, s: (0, 0, 0)),
            pl.BlockSpec((1, POOL_WIDTH), const),
            pl.BlockSpec((POOL_WIDTH, D), const, pipeline_mode=pl.Buffered(1)),
            pl.BlockSpec((1, D), const),
        ],
        out_specs=pl.BlockSpec((1, T, D), lambda b, s: (b, s, 0)),
        scratch_shapes=[pltpu.VMEM((POOL_HALO + T, POOL_WIDTH), F32)],
        compiler_params=pltpu.CompilerParams(
            dimension_semantics=("arbitrary", "arbitrary"),
            vmem_limit_bytes=VMEM_LIMIT_BYTES),
        name="pool_final" if final_norm else "pool_layer",
    )(x, g, w_in, w_mix, scale, w_out, final_g)


def kernel(x, norm_g, attn_w_in, attn_sinks, attn_w_out, pool_w_in, pool_w_mix,
           pool_scale, pool_w_out, final_g):
    depth = norm_g.shape[0]
    assert depth % 2 == 0, "the final rmsnorm is fused into the last (pooling) layer"
    attn_w_in = attn_w_in.astype(BF16)
    attn_w_out = attn_w_out.astype(BF16)
    pool_w_in = pool_w_in.astype(BF16)
    pool_w_mix = pool_w_mix.astype(BF16)
    pool_w_out = pool_w_out.astype(BF16)
    fg = final_g.reshape(1, D_MODEL)
    for i in range(depth):
        g = norm_g[i].reshape(1, D_MODEL)
        j = i // 2
        if i % 2 == 0:
            x = _attn_layer(x, g, attn_w_in[j], attn_sinks[j], attn_w_out[j])
        else:
            x = _pool_layer(x, g, pool_w_in[j], pool_w_mix[j],
                            pool_scale[j].reshape(1, POOL_WIDTH), pool_w_out[j], fg,
                            final_norm=(i == depth - 1))
    return x
```

```python
import functools
import math

import jax
import jax.numpy as jnp
from jax import lax
from jax.experimental import pallas as pl
from jax.experimental.pallas import tpu as pltpu

D_MODEL = 1024
N_HEADS = 16
N_KV_HEADS = 4
HEAD_DIM = 64
GROUP = N_HEADS // N_KV_HEADS
ATTN_WIDTH = N_HEADS * HEAD_DIM
KV_WIDTH = N_KV_HEADS * HEAD_DIM
WINDOW = 128
ATTN_IN = 2 * ATTN_WIDTH + 2 * KV_WIDTH
POOL_WIDTH = D_MODEL
POOL_WINDOWS = (2, 4, 8, 16)
POOL_GC = POOL_WIDTH // len(POOL_WINDOWS)
POOL_IN = 2 * POOL_WIDTH
POOL_HALO = 16
EPS = 1e-6
LOG2_E = math.log2(math.e)

LANES = 128
F32_SUBLANES = 8
BF16_SUBLANES = 16
V_ROWS = 2 * HEAD_DIM
STAGE_DIST = 2
POOL_SUB = 512
OUT_ROWS = 256
SEQ_TILE = 1024
VMEM_LIMIT_BYTES = 56 * 1024 * 1024

F32 = jnp.float32
BF16 = jnp.bfloat16


def _rmsnorm(x, g):
    ms = jnp.mean(x * x, axis=-1, keepdims=True)
    return (x * lax.rsqrt(ms + EPS)) * g


def _silu(z):
    return z * (1.0 / (1.0 + jnp.exp2(z * (-LOG2_E))))


def _dot(a, b):
    return jnp.dot(a, b, preferred_element_type=F32)


def _dot_nt(a, b):
    return lax.dot_general(a, b, (((1,), (1,)), ((), ())), preferred_element_type=F32)


def _attn_kernel(x_ref, g_ref, win_ref, sink_ref, wout_ref, o_ref,
                 q_s, z_s, klo_s, khi_s, vtlo_s, vthi_s, og_s, s_buf, p_buf):
    T = SEQ_TILE
    n_blocks = T // WINDOW
    s_idx = pl.program_id(1)

    @pl.when(s_idx == 0)
    def _():
        zeros = jnp.zeros((N_KV_HEADS, WINDOW, LANES), BF16)
        klo_s[:, 0:WINDOW, :] = zeros
        khi_s[:, 0:WINDOW, :] = zeros
        vtlo_s[:, 0:V_ROWS, 0:WINDOW] = zeros
        vthi_s[:, 0:V_ROWS, 0:WINDOW] = zeros

    aug_row = lax.broadcasted_iota(jnp.int32, (N_KV_HEADS, BF16_SUBLANES, WINDOW + T), 1)
    vtlo_s[:, V_ROWS:V_ROWS + BF16_SUBLANES, :] = jnp.where(aug_row == 0, 1.0, 0.0).astype(BF16)
    vthi_s[:, V_ROWS:V_ROWS + BF16_SUBLANES, :] = jnp.where(aug_row == 1, 1.0, 0.0).astype(BF16)

    x = x_ref[0]
    h = _rmsnorm(x, g_ref[...]).astype(BF16)

    scale = LOG2_E / math.sqrt(HEAD_DIM)
    n_cols = ATTN_WIDTH // LANES
    q_t = (_dot(h, win_ref[:, 0:ATTN_WIDTH]) * scale).T
    for c in range(n_cols):
        q_s[c] = q_t[c * LANES:(c + 1) * LANES, :].astype(BF16)
    z = _dot(h, win_ref[:, ATTN_WIDTH + 2 * KV_WIDTH:ATTN_IN])
    for c in range(n_cols):
        z_s[c] = _silu(z[:, c * LANES:(c + 1) * LANES])

    low_half_t = lax.broadcasted_iota(jnp.int32, (T, LANES), 1) < HEAD_DIM
    k = _dot(h, win_ref[:, ATTN_WIDTH:ATTN_WIDTH + KV_WIDTH])
    for g in range(N_KV_HEADS):
        col = k[:, (g // 2) * LANES:(g // 2 + 1) * LANES]
        if g % 2 == 0:
            lo = jnp.where(low_half_t, col, 0.0)
            hi = pltpu.roll(lo, HEAD_DIM, 1)
        else:
            hi = jnp.where(low_half_t, 0.0, col)
            lo = pltpu.roll(hi, HEAD_DIM, 1)
        klo_s[g, WINDOW:WINDOW + T, :] = lo.astype(BF16)
        khi_s[g, WINDOW:WINDOW + T, :] = hi.astype(BF16)

    v_t = _dot(h, win_ref[:, ATTN_WIDTH + KV_WIDTH:ATTN_WIDTH + 2 * KV_WIDTH]).T
    zeros_half = jnp.zeros((HEAD_DIM, T), BF16)
    for g in range(N_KV_HEADS):
        vg = v_t[g * HEAD_DIM:(g + 1) * HEAD_DIM, :].astype(BF16)
        vtlo_s[g, 0:V_ROWS, WINDOW:WINDOW + T] = jnp.concatenate([vg, zeros_half], axis=0)
        vthi_s[g, 0:V_ROWS, WINDOW:WINDOW + T] = jnp.concatenate([zeros_half, vg], axis=0)

    neg_inf = -jnp.inf

    def stage_a(r0, g, slot):
        qt = jnp.concatenate([q_s[2 * g, :, pl.ds(r0, WINDOW)],
                              q_s[2 * g + 1, :, pl.ds(r0, WINDOW)]], axis=1)
        for b, k_s in enumerate((klo_s, khi_s)):
            s_buf[slot, b] = _dot(k_s[g, pl.ds(r0, 2 * WINDOW), :], qt)

    def stage_b(g, slot, seq_start):
        key = lax.broadcasted_iota(jnp.int32, (WINDOW, WINDOW), 0)
        qry = lax.broadcasted_iota(jnp.int32, (WINDOW, WINDOW), 1)
        prev_valid = key > qry
        if seq_start is not None:
            prev_valid = prev_valid & jnp.logical_not(seq_start)
        cur_valid = key <= qry
        key0 = lax.broadcasted_iota(jnp.int32, (F32_SUBLANES, WINDOW), 0) == 0
        for b in range(2):
            for a in range(2):
                sink = sink_ref[GROUP * g + 2 * a + b] * LOG2_E
                s = s_buf[slot, b, :, a * WINDOW:(a + 1) * WINDOW]
                s0 = jnp.where(prev_valid, s[:WINDOW], neg_inf)
                s0 = jnp.concatenate([jnp.where(key0, sink, s0[:F32_SUBLANES]),
                                      s0[F32_SUBLANES:]], axis=0)
                s1 = jnp.where(cur_valid, s[WINDOW:], neg_inf)
                m = jnp.max(jnp.maximum(s0, s1), axis=0, keepdims=True)
                p = jnp.concatenate([jnp.exp2(s0 - m), jnp.exp2(s1 - m)], axis=0)
                p_buf[slot, b, :, a * WINDOW:(a + 1) * WINDOW] = p.astype(BF16)

    def stage_c(r0, g, slot):
        m_rows = V_ROWS + BF16_SUBLANES
        sink_key = ((lax.broadcasted_iota(jnp.int32, (m_rows, WINDOW), 1) == 0) &
                    (lax.broadcasted_iota(jnp.int32, (m_rows, WINDOW), 0) < V_ROWS))
        r = None
        for b, vt_s in enumerate((vtlo_s, vthi_s)):
            first = vt_s[g, :, pl.ds(r0, WINDOW)]
            first = jnp.where(sink_key, jnp.zeros_like(first), first)
            lhs = jnp.concatenate([first, vt_s[g, :, pl.ds(r0 + WINDOW, WINDOW)]], axis=1)
            part = _dot(lhs, p_buf[slot, b])
            r = part if r is None else r + part
        inv = 1.0 / r[V_ROWS:V_ROWS + F32_SUBLANES, :]
        o_t = jnp.concatenate([r[0:HEAD_DIM] * inv[0:1], r[HEAD_DIM:V_ROWS] * inv[1:2]], axis=0)
        for a in range(2):
            c = 2 * g + a
            o = o_t[:, a * WINDOW:(a + 1) * WINDOW].T
            gate = z_s[c, pl.ds(r0, WINDOW), :]
            og_s[c, pl.ds(r0, WINDOW), :] = (o * gate).astype(BF16)

    def block(i, first, last):
        if isinstance(i, int):
            r0 = i * WINDOW
        else:
            r0 = pl.multiple_of(i * WINDOW, WINDOW)
        seq_start = (s_idx == 0) if first else None
        for g in range(N_KV_HEADS):
            ga = g + STAGE_DIST
            if ga < N_KV_HEADS:
                stage_a(r0, ga, ga)
            elif not last:
                stage_a(r0 + WINDOW, ga - N_KV_HEADS, ga - N_KV_HEADS)
            stage_b(g, g, seq_start)
            gc = g - STAGE_DIST
            if gc >= 0:
                stage_c(r0, gc, gc)
            elif not first:
                stage_c(r0 - WINDOW, gc + N_KV_HEADS, gc + N_KV_HEADS)

    for g in range(STAGE_DIST):
        stage_a(0, g, g)
    block(0, True, n_blocks == 1)
    for i in range(1, n_blocks - 1):
        block(i, False, False)
    if n_blocks > 1:
        block(n_blocks - 1, False, True)
    for g in range(N_KV_HEADS - STAGE_DIST, N_KV_HEADS):
        stage_c((n_blocks - 1) * WINDOW, g, g)

    for buf in (klo_s, khi_s):
        buf[:, 0:WINDOW, :] = buf[:, T:T + WINDOW, :]
    for buf in (vtlo_s, vthi_s):
        buf[:, 0:V_ROWS, 0:WINDOW] = buf[:, 0:V_ROWS, T:T + WINDOW]

    og = jnp.concatenate([og_s[c] for c in range(n_cols)], axis=1)
    o_ref[0] = x_ref[0] + _dot(og, wout_ref[...])


def _attn_layer(x, g, w_in, sinks, w_out):
    B, S, D = x.shape
    T = SEQ_TILE
    const = lambda b, s: (0, 0)
    return pl.pallas_call(
        _attn_kernel,
        out_shape=jax.ShapeDtypeStruct((B, S, D), F32),
        grid=(B, S // T),
        in_specs=[
            pl.BlockSpec((1, T, D), lambda b, s: (b, s, 0)),
            pl.BlockSpec((1, D), const),
            pl.BlockSpec((D, ATTN_IN), const, pipeline_mode=pl.Buffered(1)),
            pl.BlockSpec(memory_space=pltpu.SMEM),
            pl.BlockSpec((ATTN_WIDTH, D), const, pipeline_mode=pl.Buffered(1)),
        ],
        out_specs=pl.BlockSpec((1, T, D), lambda b, s: (b, s, 0)),
        scratch_shapes=[
            pltpu.VMEM((ATTN_WIDTH // LANES, LANES, T), BF16),
            pltpu.VMEM((ATTN_WIDTH // LANES, T, LANES), F32),
            pltpu.VMEM((N_KV_HEADS, WINDOW + T, LANES), BF16),
            pltpu.VMEM((N_KV_HEADS, WINDOW + T, LANES), BF16),
            pltpu.VMEM((N_KV_HEADS, V_ROWS + BF16_SUBLANES, WINDOW + T), BF16),
            pltpu.VMEM((N_KV_HEADS, V_ROWS + BF16_SUBLANES, WINDOW + T), BF16),
            pltpu.VMEM((ATTN_WIDTH // LANES, T, LANES), BF16),
            pltpu.VMEM((N_KV_HEADS, 2, 2 * WINDOW, 2 * WINDOW), F32),
            pltpu.VMEM((N_KV_HEADS, 2, 2 * WINDOW, 2 * WINDOW), BF16),
        ],
        compiler_params=pltpu.CompilerParams(
            dimension_semantics=("arbitrary", "arbitrary"),
            vmem_limit_bytes=VMEM_LIMIT_BYTES),
        name="attn_layer",
    )(x, g, w_in, sinks, w_out)


def _pool_kernel(x_ref, g_ref, win_ref, wmix_ref, scale_ref, wout_ref, fg_ref, o_ref,
                 u_s, *, final_norm):
    T = SEQ_TILE
    s_idx = pl.program_id(1)

    @pl.when(s_idx == 0)
    def _():
        u_s[0:POOL_HALO, :] = jnp.zeros((POOL_HALO, POOL_WIDTH), F32)

    x = x_ref[0]
    h = _rmsnorm(x, g_ref[...]).astype(BF16)

    assert T == 2 * POOL_SUB
    n_groups = len(POOL_WINDOWS)
    head_pos = lax.broadcasted_iota(jnp.int32, (POOL_HALO, 1), 0)

    def cols(gi):
        return slice(gi * POOL_GC, (gi + 1) * POOL_GC)

    def u_proj(k, gi):
        rows = slice(POOL_HALO + k * POOL_SUB, POOL_HALO + (k + 1) * POOL_SUB)
        u_s[rows, cols(gi)] = _dot(h[k * POOL_SUB:(k + 1) * POOL_SUB], win_ref[:, cols(gi)])

    def z_proj(k, gi):
        c0 = POOL_WIDTH + gi * POOL_GC
        return _dot(h[k * POOL_SUB:(k + 1) * POOL_SUB], win_ref[:, c0:c0 + POOL_GC])

    def pool(k, gi):
        w = POOL_WINDOWS[gi]
        r0 = k * POOL_SUB
        ue = u_s[r0:r0 + POOL_HALO + POOL_SUB, cols(gi)]
        acc = ue
        step = 1
        while step < w:
            acc = acc + pltpu.roll(acc, step, 0)
            step *= 2
        sums = acc[POOL_HALO:]
        if k == 0:
            count = jnp.minimum(head_pos + 1, w).astype(F32)
            head_factor = jnp.where(s_idx == 0, float(w) / count, 1.0)
            sums = jnp.concatenate([sums[0:POOL_HALO] * head_factor, sums[POOL_HALO:]], axis=0)
        pooled = sums * (1.0 / w) - ue[POOL_HALO:]
        return pooled.astype(BF16)

    def mix(gi, pooled):
        return _dot(pooled, wmix_ref[gi]) * scale_ref[:, cols(gi)]

    def gate(mixed, z):
        return (mixed * _silu(z)).astype(BF16)

    def out_proj(k, gated, r):
        rows = slice(k * POOL_SUB + r, k * POOL_SUB + r + OUT_ROWS)
        out = x_ref[0, rows, :] + _dot(gated[r:r + OUT_ROWS], wout_ref[...])
        if final_norm:
            out = _rmsnorm(out, fg_ref[...])
        o_ref[0, rows, :] = out

    for gi in range(n_groups):
        u_proj(0, gi)
    z0, pooled0, mixed0 = [], [], []
    for gi in range(n_groups):
        z0.append(z_proj(0, gi))
        pooled0.append(pool(0, gi))
        if gi > 0:
            mixed0.append(mix(gi - 1, pooled0[gi - 1]))
    gated0 = []
    for gi in range(n_groups):
        u_proj(1, gi)
        if gi == 0:
            mixed0.append(mix(n_groups - 1, pooled0[n_groups - 1]))
        gated0.append(gate(mixed0[gi], z0[gi]))
    gated0 = jnp.concatenate(gated0, axis=1)
    z1, pooled1, mixed1 = [], [], []
    for gi in range(n_groups):
        z1.append(z_proj(1, gi))
        pooled1.append(pool(1, gi))
        if gi > 0:
            mixed1.append(mix(gi - 1, pooled1[gi - 1]))
    gated1 = []
    chunks = list(range(0, POOL_SUB, OUT_ROWS))
    per_chunk = n_groups // len(chunks)
    for ci, r in enumerate(chunks):
        out_proj(0, gated0, r)
        if ci == 0:
            mixed1.append(mix(n_groups - 1, pooled1[n_groups - 1]))
        for gi in range(ci * per_chunk, (ci + 1) * per_chunk):
            gated1.append(gate(mixed1[gi], z1[gi]))
    gated1 = jnp.concatenate(gated1, axis=1)
    for r in chunks:
        out_proj(1, gated1, r)

    u_s[0:POOL_HALO, :] = u_s[T:T + POOL_HALO, :]


def _pool_layer(x, g, w_in, w_mix, scale, w_out, final_g, final_norm):
    B, S, D = x.shape
    T = SEQ_TILE
    const = lambda b, s: (0, 0)
    return pl.pallas_call(
        functools.partial(_pool_kernel, final_norm=final_norm),
        out_shape=jax.ShapeDtypeStruct((B, S, D), F32),
        grid=(B, S // T),
        in_specs=[
            pl.BlockSpec((1, T, D), lambda b, s: (b, s, 0)),
            pl.BlockSpec((1, D), const),
            pl.BlockSpec((D, POOL_IN), const, pipeline_mode=pl.Buffered(1)),
            pl.BlockSpec((len(POOL_WINDOWS), POOL_GC, POOL_GC), lambda b, s: (0, 0, 0)),
            pl.BlockSpec((1, POOL_WIDTH), const),
            pl.BlockSpec((POOL_WIDTH, D), const, pipeline_mode=pl.Buffered(1)),
            pl.BlockSpec((1, D), const),
        ],
        out_specs=pl.BlockSpec((1, T, D), lambda b, s: (b, s, 0)),
        scratch_shapes=[pltpu.VMEM((POOL_HALO + T, POOL_WIDTH), F32)],
        compiler_params=pltpu.CompilerParams(
            dimension_semantics=("arbitrary", "arbitrary"),
            vmem_limit_bytes=VMEM_LIMIT_BYTES),
        name="pool_final" if final_norm else "pool_layer",
    )(x, g, w_in, w_mix, scale, w_out, final_g)


def kernel(x, norm_g, attn_w_in, attn_sinks, attn_w_out, pool_w_in, pool_w_mix,
           pool_scale, pool_w_out, final_g):
    depth = norm_g.shape[0]
    assert depth % 2 == 0, "the final rmsnorm is fused into the last (pooling) layer"
    attn_w_in = attn_w_in.astype(BF16)
    attn_w_out = attn_w_out.astype(BF16)
    pool_w_in = pool_w_in.astype(BF16)
    pool_w_mix = pool_w_mix.astype(BF16)
    pool_w_out = pool_w_out.astype(BF16)
    fg = final_g.reshape(1, D_MODEL)
    for i in range(depth):
        g = norm_g[i].reshape(1, D_MODEL)
        j = i // 2
        if i % 2 == 0:
            x = _attn_layer(x, g, attn_w_in[j], attn_sinks[j], attn_w_out[j])
        else:
            x = _pool_layer(x, g, pool_w_in[j], pool_w_mix[j],
                            pool_scale[j].reshape(1, POOL_WIDTH), pool_w_out[j], fg,
                            final_norm=(i == depth - 1))
    return x
```
